```python
import math
import jax, jax.numpy as jnp
from jax import lax
import numpy as np

D_MODEL = 2048
BATCH = 1
SEQ = 8192
DEPTH = 2
DEC_BATCH = 128
DEC_SEQ = 8
PAST_LEN = 8192
PAGE_SIZE = 128

N_AB = (DEPTH + 1) // 2
N_C = DEPTH // 2
NORM_EPS = 1e-6
QBLK = 128
A_HEADS = 16
A_KV = 2
A_HD = 64
A_WIN = 128
B_HEADS = 4
B_DK = 128
B_DV = 256
B_CHUNK = 64
C_HEADS = 16
C_KV = 4
C_HD = 128
C_CMP = 32
C_SLC = 64
C_TOPN = 16
C_WIN = 512
P_HEADS = 8
P_NKEYS = 128
P_EXPERTS = P_NKEYS * P_NKEYS
P_DKH = 128
P_TOPK = 16
P_BLOCK = 128

A_Q = A_HEADS * A_HD
A_K = A_KV * A_HD
B_Q = B_HEADS * B_DK
B_V = B_HEADS * B_DV
AB_SPLITS = (A_Q, A_K, A_K, B_Q, B_Q, B_V, B_V, B_HEADS, B_HEADS)
AB_IN = sum(AB_SPLITS)
AB_MIX = A_Q + B_V
C_Q = C_HEADS * C_HD
C_K = C_KV * C_HD
C_SPLITS = (C_Q, C_K, C_K, C_K, C_K, C_K, C_K, 3 * C_HEADS)
C_IN = sum(C_SPLITS)

kernel_name = 'hybrid_swa_mlstm_nsa_peer_step'


def rmsnorm(x, g):
    xf = x.astype(jnp.float32)
    y = xf * lax.rsqrt(jnp.mean(xf * xf, axis=-1, keepdims=True) + NORM_EPS)
    return (y * g.astype(jnp.float32)).astype(x.dtype)


def split_cols(h, sizes):
    out, o = [], 0
    for n in sizes:
        out.append(h[..., o:o + n])
        o += n
    return out


def masked_softmax(s, mask):
    s = jnp.where(mask, s, -jnp.inf)
    m = jnp.max(s, axis=-1, keepdims=True)
    m = jnp.where(jnp.isfinite(m), m, 0.0)
    p = jnp.exp(s - m)
    d = jnp.sum(p, axis=-1, keepdims=True)
    return p / jnp.where(d > 0, d, 1.0)


def window_mask(qpos, kpos, window):
    d = qpos[..., :, None] - kpos[..., None, :]
    return (d >= 0) & (d <= window) & (kpos[..., None, :] >= 0)


def band_blocks(k, window, block):
    bx, s = k.shape[0], k.shape[1]
    nb, ns = s // block, window // block
    kp = jnp.pad(k, ((0, 0), (window, 0)) + ((0, 0),) * (k.ndim - 2))
    kb = kp.reshape(bx, nb + ns, block, *k.shape[2:])
    return jnp.concatenate([kb[:, i:i + nb] for i in range(ns + 1)], axis=2)


def band_positions(nb, window, block):
    base = jnp.arange(nb)[:, None] * block
    return base + jnp.arange(block)[None, :], base - window + jnp.arange(window + block)[None, :]


def sink_attention(q, k, v, mask, sinks):
    s = jnp.einsum('...tgrd,...sgd->...grts', q, k).astype(jnp.float32) * (A_HD ** -0.5)
    s = jnp.where(mask[..., None, None, :, :], s, -jnp.inf)
    sk = sinks.astype(jnp.float32)[:, :, None, None]
    m = jnp.maximum(jnp.max(s, axis=-1, keepdims=True), sk)
    p = jnp.exp(s - m)
    p = p / (jnp.sum(p, axis=-1, keepdims=True) + jnp.exp(sk - m))
    return jnp.einsum('...grts,...sgd->...tgrd', p.astype(v.dtype), v)


def mlstm_chunked(q, k, v, ig, fg, c0, n0, m0):
    f32 = jnp.float32
    bx, t = q.shape[0], q.shape[1]
    L = math.gcd(t, B_CHUNK)
    nc = t // L

    def chunks(a):
        return jnp.moveaxis(a.reshape(bx, nc, L, *a.shape[2:]), 1, 0)

    xs = (chunks(q.astype(f32)), chunks(k.astype(f32) * (B_DK ** -0.5)), chunks(v.astype(f32)),
          chunks(ig.astype(f32)), chunks(jax.nn.log_sigmoid(fg.astype(f32))))
    causal = jnp.tril(jnp.ones((L, L), dtype=bool))

    def step(carry, inp):
        c, n, m = carry
        qc, kc, vc, li, lf = inp
        b = jnp.cumsum(lf, axis=1)
        dlog = b[:, :, None, :] - b[:, None, :, :] + li[:, None, :, :]
        dlog = jnp.where(causal[None, :, :, None], dlog, -jnp.inf)
        inter = b + m[:, None, :]
        mrow = jnp.maximum(inter, jnp.max(dlog, axis=2))
        w = jnp.exp(dlog - mrow[:, :, None, :])
        a = jnp.exp(inter - mrow)
        sw = jnp.einsum('bthd,bjhd->btjh', qc, kc) * w
        num = jnp.einsum('btjh,bjhv->bthv', sw, vc) + a[..., None] * jnp.einsum('bthd,bhdv->bthv', qc, c)
        den = jnp.sum(sw, axis=2) + a * jnp.einsum('bthd,bhd->bth', qc, n)
        h = num / jnp.maximum(jnp.abs(den), jnp.exp(-mrow))[..., None]
        bl = b[:, -1]
        gj = bl[:, None, :] - b + li
        m_new = jnp.maximum(bl + m, jnp.max(gj, axis=1))
        wj = jnp.exp(gj - m_new[:, None, :])
        decay = jnp.exp(bl + m - m_new)
        c_new = decay[..., None, None] * c + jnp.einsum('bjh,bjhd,bjhv->bhdv', wj, kc, vc)
        n_new = decay[..., None] * n + jnp.einsum('bjh,bjhd->bhd', wj, kc)
        return (c_new, n_new, m_new), h

    (c, n, m), hs = lax.scan(step, (c0.astype(f32), n0.astype(f32), m0.astype(f32)), xs)
    return jnp.moveaxis(hs, 0, 1).reshape(bx, t, B_HEADS, B_DV), c, n, m


def ab_project(h, w_in, b_i, b_f):
    bx, t = h.shape[0], h.shape[1]
    qa, ka, va, qb, kb, vb, ob, ib, fb = split_cols(h @ w_in, AB_SPLITS)
    qa = qa.reshape(bx, t, A_KV, A_HEADS // A_KV, A_HD)
    ka = ka.reshape(bx, t, A_KV, A_HD)
    va = va.reshape(bx, t, A_KV, A_HD)
    qb = qb.reshape(bx, t, B_HEADS, B_DK)
    kb = kb.reshape(bx, t, B_HEADS, B_DK)
    vb = vb.reshape(bx, t, B_HEADS, B_DV)
    return qa, ka, va, qb, kb, vb, ob, ib + b_i, fb + b_f


def ab_output(oa, hb, ob, hnorm, w_out):
    bx, t = oa.shape[0], oa.shape[1]
    hn = hb * lax.rsqrt(jnp.mean(hb * hb, axis=-1, keepdims=True) + NORM_EPS)
    hn = hn.reshape(bx, t, B_V) * hnorm.astype(jnp.float32) * jax.nn.sigmoid(ob.astype(jnp.float32))
    mix = jnp.concatenate([oa.reshape(bx, t, A_Q), hn.astype(oa.dtype)], axis=-1)
    return mix @ w_out


def mixer_ab(hp, hs, swa_k, swa_v, c0, n0, m0, w_in, w_out, sinks, b_i, b_f, hnorm):
    sinks = sinks.reshape(A_KV, A_HEADS // A_KV)
    qa, ka, va, qb, kb, vb, ob, ib, fb = ab_project(hp, w_in, b_i, b_f)
    bx, s = hp.shape[0], hp.shape[1]
    nb = s // QBLK
    qpos, kpos = band_positions(nb, A_WIN, QBLK)
    oa = sink_attention(qa.reshape(bx, nb, QBLK, A_KV, A_HEADS // A_KV, A_HD),
                        band_blocks(ka, A_WIN, QBLK), band_blocks(va, A_WIN, QBLK),
                        window_mask(qpos, kpos, A_WIN)[None], sinks)
    hb, cp, np_, mp = mlstm_chunked(qb, kb, vb, ib, fb,
                                    jnp.zeros((bx, B_HEADS, B_DK, B_DV), jnp.float32),
                                    jnp.zeros((bx, B_HEADS, B_DK), jnp.float32),
                                    jnp.zeros((bx, B_HEADS), jnp.float32))
    y_p = ab_output(oa.reshape(bx, s, A_Q), hb, ob, hnorm, w_out)
    wbp = min(A_WIN, s)
    qa2, ka2, va2, qb2, kb2, vb2, ob2, ib2, fb2 = ab_project(hs, w_in, b_i, b_f)
    db, t = hs.shape[0], hs.shape[1]
    wb = swa_k.shape[1]
    k_all = jnp.concatenate([swa_k, ka2], axis=1)
    v_all = jnp.concatenate([swa_v, va2], axis=1)
    kpos2 = PAST_LEN - wb + jnp.arange(wb + t)
    qpos2 = PAST_LEN + jnp.arange(t)
    oa2 = sink_attention(qa2, k_all, v_all, window_mask(qpos2, kpos2, A_WIN)[None], sinks)
    hb2, cs, ns, ms = mlstm_chunked(qb2, kb2, vb2, ib2, fb2, c0, n0, m0)
    y_s = ab_output(oa2.reshape(db, t, A_Q), hb2, ob2, hnorm, w_out)
    dp, ds = hp.dtype, hs.dtype
    new = (ka[:, s - wbp:], va[:, s - wbp:], k_all[:, -wb:], v_all[:, -wb:],
           cp.astype(dp), np_.astype(dp), mp.astype(dp), cs.astype(ds), ns.astype(ds), ms.astype(ds))
    return y_p, y_s, new


def c_project(h, w_in):
    bx, t = h.shape[0], h.shape[1]
    q, kc, vc, ks, vs, kw, vw, g = split_cols(h @ w_in, C_SPLITS)

    def kv(a):
        return a.reshape(bx, t, C_KV, C_HD)

    q = q.reshape(bx, t, C_KV, C_HEADS // C_KV, C_HD)
    g = jax.nn.sigmoid(g.astype(jnp.float32)).reshape(bx, t, C_KV, C_HEADS // C_KV, 3)
    return q, kv(kc), kv(vc), kv(ks), kv(vs), kv(kw), kv(vw), g


def compress(raw, pe, w1, w2):
    nc = raw.shape[0] // C_CMP
    blk = raw[:nc * C_CMP].reshape(nc, C_CMP, C_KV, C_HD) + pe[None, :, None, :]
    flat = jnp.swapaxes(blk, 1, 2).reshape(nc, C_KV, C_CMP * C_HD)
    return jax.nn.gelu(flat @ w1) @ w2


def nsa_block(q, qpos, kc, vc, gather, nsb, kw, vw, kwpos, gate):
    f32 = jnp.float32
    t, g, r = q.shape[0], q.shape[1], q.shape[2]
    scale = C_HD ** -0.5
    nc = kc.shape[0]
    cend = (jnp.arange(nc) + 1) * C_CMP - 1
    cmask = (cend[None, :] <= qpos[:, None])[:, None, None, :]
    pc = masked_softmax(jnp.einsum('tgrd,cgd->tgrc', q, kc).astype(f32) * scale, cmask)
    oc = jnp.einsum('tgrc,cgd->tgrd', pc.astype(vc.dtype), vc)
    cps = C_SLC // C_CMP
    imp = jnp.pad(pc.sum(axis=2), ((0, 0), (0, 0), (0, nsb * cps - nc)))
    imp = imp.reshape(t, g, nsb, cps).sum(axis=-1)
    jb = jnp.arange(nsb)[None, :]
    cur = (qpos // C_SLC)[:, None]
    forced = (jb == 0) | (jb == cur) | (jb == cur - 1)
    score = jnp.where(forced[:, None, :], jnp.inf, imp)
    score = jnp.where((jb <= cur)[:, None, :], score, -jnp.inf)
    _, idx = lax.top_k(score, min(C_TOPN, nsb))
    ks, vs = gather(idx)
    n = idx.shape[-1]
    kpos = idx[..., None] * C_SLC + jnp.arange(C_SLC)
    smask = (kpos <= qpos[:, None, None, None]).reshape(t, g, 1, n * C_SLC)
    ss = jnp.einsum('tgrd,tgnld->tgrnl', q, ks).astype(f32).reshape(t, g, r, n * C_SLC) * scale
    ps = masked_softmax(ss, smask)
    osl = jnp.einsum('tgrm,tgmd->tgrd', ps.astype(vs.dtype), vs.reshape(t, g, n * C_SLC, C_HD))
    wmask = window_mask(qpos, kwpos, C_WIN)[:, None, None, :]
    pw = masked_softmax(jnp.einsum('tgrd,wgd->tgrw', q, kw).astype(f32) * scale, wmask)
    ow = jnp.einsum('tgrw,wgd->tgrd', pw.astype(vw.dtype), vw)
    gt = gate.astype(q.dtype)
    return gt[..., 0:1] * oc + gt[..., 1:2] * osl + gt[..., 2:3] * ow


def mixer_c(hp, hs, c, page_table, pool_ck, pool_cv, pool_sk, pool_sv, win_k, win_v,
            w_in, w_out, pe_k, w1_k, w2_k, pe_v, w1_v, w2_v):
    gidx = jnp.arange(C_KV)[None, :, None, None]
    q, kcr, vcr, ksl, vsl, kw, vw, g = c_project(hp, w_in)
    bx, s = hp.shape[0], hp.shape[1]
    nb = s // QBLK
    qposb, kposb = band_positions(nb, C_WIN, QBLK)
    kwb = band_blocks(kw, C_WIN, QBLK)
    vwb = band_blocks(vw, C_WIN, QBLK)

    def prompt_seq(args):
        q1, kcr1, vcr1, ks1, vs1, kwb1, vwb1, g1 = args
        kc = compress(kcr1, pe_k, w1_k, w2_k)
        vc = compress(vcr1, pe_v, w1_v, w2_v)
        nsb = s // C_SLC

        def gather(idx):
            rows = idx[..., None] * C_SLC + jnp.arange(C_SLC)
            return ks1[rows, gidx], vs1[rows, gidx]

        def blockfn(xs):
            qq, qp, kk, vv, kp, gg = xs
            return nsa_block(qq, qp, kc, vc, gather, nsb, kk, vv, kp, gg)

        o = lax.map(blockfn, (q1.reshape(nb, QBLK, *q1.shape[1:]), qposb, kwb1, vwb1, kposb,
                              g1.reshape(nb, QBLK, *g1.shape[1:])))
        return o.reshape(s, *q1.shape[1:])

    o_p = lax.map(prompt_seq, (q, kcr, vcr, ksl, vsl, kwb, vwb, g))
    y_p = o_p.reshape(bx, s, C_Q) @ w_out
    q2, kcr2, vcr2, ksl2, vsl2, kw2, vw2, g2 = c_project(hs, w_in)
    db, t = hs.shape[0], hs.shape[1]
    L = PAST_LEN + t
    past_blocks = PAST_LEN // C_SLC
    nsb = -(-L // C_SLC)
    nbn = nsb - past_blocks
    wb = win_k.shape[1]
    kw_all = jnp.concatenate([win_k, kw2], axis=1)
    vw_all = jnp.concatenate([win_v, vw2], axis=1)
    kwpos = PAST_LEN - wb + jnp.arange(wb + t)
    qpos = PAST_LEN + jnp.arange(t)

    def sample_seq(args):
        pt, q1, kcr1, vcr1, ks1, vs1, kw1, vw1, g1 = args
        raw_k = jnp.concatenate([pool_ck[c, pt].reshape(PAST_LEN, C_KV, C_HD), kcr1], axis=0)
        raw_v = jnp.concatenate([pool_cv[c, pt].reshape(PAST_LEN, C_KV, C_HD), vcr1], axis=0)
        kc = compress(raw_k, pe_k, w1_k, w2_k)
        vc = compress(raw_v, pe_v, w1_v, w2_v)
        pad = ((0, nbn * C_SLC - t), (0, 0), (0, 0))
        ksn = jnp.pad(ks1, pad)
        vsn = jnp.pad(vs1, pad)

        def gather(idx):
            off = jnp.arange(C_SLC)
            is_past = (idx < past_blocks)[..., None, None]
            start = jnp.minimum(idx, past_blocks - 1) * C_SLC
            phys = pt[start // PAGE_SIZE][..., None]
            slot = (start % PAGE_SIZE)[..., None] + off
            rows_new = jnp.clip(idx - past_blocks, 0, nbn - 1)[..., None] * C_SLC + off
            kk = jnp.where(is_past, pool_sk[c, phys, slot, gidx], ksn[rows_new, gidx])
            vv = jnp.where(is_past, pool_sv[c, phys, slot, gidx], vsn[rows_new, gidx])
            return kk, vv

        return nsa_block(q1, qpos, kc, vc, gather, nsb, kw1, vw1, kwpos, g1)

    o_s = lax.map(sample_seq, (page_table, q2, kcr2, vcr2, ksl2, vsl2, kw_all, vw_all, g2))
    y_s = o_s.reshape(db, t, C_Q) @ w_out
    wbp = min(C_WIN, s)
    new = (kcr, vcr, ksl, vsl, kcr2, vcr2, ksl2, vsl2,
           kw[:, s - wbp:], vw[:, s - wbp:], kw_all[:, -wb:], vw_all[:, -wb:])
    return y_p, y_s, new


def peer(x, w_q, keys1, keys2, u_tab, v_tab):
    n_tok = x.shape[0]
    pad = (-n_tok) % P_BLOCK
    xb_all = jnp.pad(x, ((0, pad), (0, 0))).reshape(-1, P_BLOCK, D_MODEL)

    def blk(xb):
        q = (xb @ w_q).reshape(P_BLOCK, P_HEADS, 2, P_DKH)
        s1 = jnp.einsum('thd,hkd->thk', q[:, :, 0], keys1).astype(jnp.float32)
        s2 = jnp.einsum('thd,hkd->thk', q[:, :, 1], keys2).astype(jnp.float32)
        v1, i1 = lax.top_k(s1, P_TOPK)
        v2, i2 = lax.top_k(s2, P_TOPK)
        cand = (v1[..., :, None] + v2[..., None, :]).reshape(P_BLOCK, P_HEADS, P_TOPK * P_TOPK)
        cidx = (i1[..., :, None] * P_NKEYS + i2[..., None, :]).reshape(P_BLOCK, P_HEADS, P_TOPK * P_TOPK)
        sv, si = lax.top_k(cand, P_TOPK)
        eidx = jnp.take_along_axis(cidx, si, axis=-1)
        gate = jax.nn.softmax(sv, axis=-1)
        act = jax.nn.gelu(jnp.einsum('thkd,td->thk', u_tab[eidx], xb).astype(jnp.float32))
        return jnp.einsum('thk,thkd->td', (gate * act).astype(xb.dtype), v_tab[eidx])

    return lax.map(blk, xb_all).reshape(-1, D_MODEL)[:n_tok]


def setup_inputs(seed: int = 0) -> dict:
    key = jax.random.key(seed)
    keys = iter(jax.random.split(key, 40))

    def nrm(shape, scale):
        return jax.random.normal(next(keys), shape, jnp.float32) * scale

    n_pages = PAST_LEN // PAGE_SIZE
    n_used = DEC_BATCH * n_pages
    n_phys = n_used + n_used // 4
    wa = min(A_WIN, PAST_LEN)
    wc = min(C_WIN, PAST_LEN)
    pool = (N_C, n_phys, PAGE_SIZE, C_KV, C_HD)
    x_prompt = nrm((BATCH, SEQ, D_MODEL), 1.0)
    x_sample = nrm((DEC_BATCH, DEC_SEQ, D_MODEL), 1.0)
    cache_swa_k = nrm((N_AB, DEC_BATCH, wa, A_KV, A_HD), 1.0)
    cache_swa_v = nrm((N_AB, DEC_BATCH, wa, A_KV, A_HD), 1.0)
    state_mlstm_c = nrm((N_AB, DEC_BATCH, B_HEADS, B_DK, B_DV), 1.0)
    state_mlstm_n = nrm((N_AB, DEC_BATCH, B_HEADS, B_DK), 1.0)
    state_mlstm_m = nrm((N_AB, DEC_BATCH, B_HEADS), 1.0)
    cache_nsa_cmp_k = nrm(pool, 1.0)
    cache_nsa_cmp_v = nrm(pool, 1.0)
    cache_nsa_slc_k = nrm(pool, 1.0)
    cache_nsa_slc_v = nrm(pool, 1.0)
    cache_nsa_win_k = nrm((N_C, DEC_BATCH, wc, C_KV, C_HD), 1.0)
    cache_nsa_win_v = nrm((N_C, DEC_BATCH, wc, C_KV, C_HD), 1.0)
    perm = jax.random.permutation(next(keys), n_phys)
    page_table = perm[:n_used].reshape(DEC_BATCH, n_pages).astype(jnp.int32)
    return {
        'x_prompt': x_prompt, 'x_sample': x_sample,
        'cache_swa_k': cache_swa_k, 'cache_swa_v': cache_swa_v,
        'state_mlstm_c': state_mlstm_c, 'state_mlstm_n': state_mlstm_n, 'state_mlstm_m': state_mlstm_m,
        'cache_nsa_cmp_k': cache_nsa_cmp_k, 'cache_nsa_cmp_v': cache_nsa_cmp_v,
        'cache_nsa_slc_k': cache_nsa_slc_k, 'cache_nsa_slc_v': cache_nsa_slc_v,
        'cache_nsa_win_k': cache_nsa_win_k, 'cache_nsa_win_v': cache_nsa_win_v,
        'page_table': page_table,
        'norm_mix': 1.0 + nrm((DEPTH, D_MODEL), 0.02),
        'norm_ffn': 1.0 + nrm((DEPTH, D_MODEL), 0.02),
        'norm_final': 1.0 + nrm((D_MODEL,), 0.02),
        'ab_w_in': nrm((N_AB, D_MODEL, AB_IN), D_MODEL ** -0.5),
        'ab_w_out': nrm((N_AB, AB_MIX, D_MODEL), AB_MIX ** -0.5),
        'ab_sinks': nrm((N_AB, A_HEADS), 0.5),
        'ab_b_igate': nrm((N_AB, B_HEADS), 0.1),
        'ab_b_fgate': 3.0 + nrm((N_AB, B_HEADS), 0.5),
        'ab_hnorm': 1.0 + nrm((N_AB, B_V), 0.02),
        'c_w_in': nrm((N_C, D_MODEL, C_IN), D_MODEL ** -0.5),
        'c_w_out': nrm((N_C, C_Q, D_MODEL), C_Q ** -0.5),
        'c_pe_k': nrm((N_C, C_CMP, C_HD), 0.1),
        'c_w1_k': nrm((N_C, C_CMP * C_HD, C_HD), (C_CMP * C_HD) ** -0.5),
        'c_w2_k': nrm((N_C, C_HD, C_HD), C_HD ** -0.5),
        'c_pe_v': nrm((N_C, C_CMP, C_HD), 0.1),
        'c_w1_v': nrm((N_C, C_CMP * C_HD, C_HD), (C_CMP * C_HD) ** -0.5),
        'c_w2_v': nrm((N_C, C_HD, C_HD), C_HD ** -0.5),
        'peer_w_q': nrm((DEPTH, D_MODEL, P_HEADS * 2 * P_DKH), D_MODEL ** -0.5),
        'peer_keys1': nrm((DEPTH, P_HEADS, P_NKEYS, P_DKH), P_DKH ** -0.5),
        'peer_keys2': nrm((DEPTH, P_HEADS, P_NKEYS, P_DKH), P_DKH ** -0.5),
        'peer_u': nrm((DEPTH, P_EXPERTS, D_MODEL), D_MODEL ** -0.5),
        'peer_v': nrm((DEPTH, P_EXPERTS, D_MODEL), P_HEADS ** -0.5),
    }


def reference(x_prompt, x_sample, cache_swa_k, cache_swa_v, state_mlstm_c, state_mlstm_n, state_mlstm_m,
              cache_nsa_cmp_k, cache_nsa_cmp_v, cache_nsa_slc_k, cache_nsa_slc_v, cache_nsa_win_k, cache_nsa_win_v,
              page_table, norm_mix, norm_ffn, norm_final, ab_w_in, ab_w_out, ab_sinks, ab_b_igate, ab_b_fgate,
              ab_hnorm, c_w_in, c_w_out, c_pe_k, c_w1_k, c_w2_k, c_pe_v, c_w1_v, c_w2_v,
              peer_w_q, peer_keys1, peer_keys2, peer_u, peer_v):
    ab_names = ('swa_k_p', 'swa_v_p', 'swa_k_s', 'swa_v_s', 'c_p', 'n_p', 'm_p', 'c_s', 'n_s', 'm_s')
    c_names = ('ck_p', 'cv_p', 'sk_p', 'sv_p', 'ck_s', 'cv_s', 'sk_s', 'sv_s', 'wk_p', 'wv_p', 'wk_s', 'wv_s')
    new = {nm: [] for nm in ab_names + c_names}
    xp, xs = x_prompt, x_sample
    for layer in range(DEPTH):
        hp = rmsnorm(xp, norm_mix[layer])
        hs = rmsnorm(xs, norm_mix[layer])
        if layer % 2 == 0:
            a = layer // 2
            yp, ys, st = mixer_ab(hp, hs, cache_swa_k[a], cache_swa_v[a], state_mlstm_c[a], state_mlstm_n[a],
                                  state_mlstm_m[a], ab_w_in[a], ab_w_out[a], ab_sinks[a], ab_b_igate[a],
                                  ab_b_fgate[a], ab_hnorm[a])
            names = ab_names
        else:
            c = layer // 2
            yp, ys, st = mixer_c(hp, hs, c, page_table, cache_nsa_cmp_k, cache_nsa_cmp_v, cache_nsa_slc_k,
                                 cache_nsa_slc_v, cache_nsa_win_k[c], cache_nsa_win_v[c], c_w_in[c], c_w_out[c],
                                 c_pe_k[c], c_w1_k[c], c_w2_k[c], c_pe_v[c], c_w1_v[c], c_w2_v[c])
            names = c_names
        for nm, arr in zip(names, st):
            new[nm].append(arr)
        xp = xp + yp
        xs = xs + ys
        hp = rmsnorm(xp, norm_ffn[layer])
        hs = rmsnorm(xs, norm_ffn[layer])
        w_q, k1, k2, u_t, v_t = peer_w_q[layer], peer_keys1[layer], peer_keys2[layer], peer_u[layer], peer_v[layer]
        xp = xp + peer(hp.reshape(-1, D_MODEL), w_q, k1, k2, u_t, v_t).reshape(xp.shape)
        xs = xs + peer(hs.reshape(-1, D_MODEL), w_q, k1, k2, u_t, v_t).reshape(xs.shape)
    y_prompt = rmsnorm(xp, norm_final)
    y_sample = rmsnorm(xs, norm_final)
    return (y_prompt, y_sample,
            jnp.stack(new['swa_k_p']), jnp.stack(new['swa_v_p']), jnp.stack(new['swa_k_s']), jnp.stack(new['swa_v_s']),
            jnp.stack(new['c_p']), jnp.stack(new['n_p']), jnp.stack(new['m_p']),
            jnp.stack(new['c_s']), jnp.stack(new['n_s']), jnp.stack(new['m_s']),
            jnp.stack(new['ck_p']), jnp.stack(new['cv_p']), jnp.stack(new['sk_p']), jnp.stack(new['sv_p']),
            jnp.stack(new['ck_s']), jnp.stack(new['cv_s']), jnp.stack(new['sk_s']), jnp.stack(new['sv_s']),
            jnp.stack(new['wk_p']), jnp.stack(new['wv_p']), jnp.stack(new['wk_s']), jnp.stack(new['wv_s']))
```

```python
import functools
import math

import jax
import jax.numpy as jnp
from jax import lax
from jax.experimental import pallas as pl
from jax.experimental.pallas import tpu as pltpu

D_MODEL = 2048
SEQ = 8192
DEPTH = 2
DEC_BATCH = 128
DEC_SEQ = 8
PAST_LEN = 8192
PAGE_SIZE = 128
NORM_EPS = 1e-6
QBLK = 128
A_HEADS, A_KV, A_HD, A_WIN = 16, 2, 64, 128
B_HEADS, B_DK, B_DV, B_CHUNK = 4, 128, 256, 64
C_HEADS, C_KV, C_HD, C_CMP, C_SLC, C_TOPN, C_WIN = 16, 4, 128, 32, 64, 16, 512
P_HEADS, P_NKEYS, P_DKH, P_TOPK = 8, 128, 128, 16
P_EXPERTS = P_NKEYS * P_NKEYS

A_Q = A_HEADS * A_HD
A_K = A_KV * A_HD
B_Q = B_HEADS * B_DK
B_V = B_HEADS * B_DV
AB_SPLITS = (A_Q, A_K, A_K, B_Q, B_Q, B_V, B_V, B_HEADS, B_HEADS)
AB_IN = sum(AB_SPLITS)
C_Q = C_HEADS * C_HD
C_K = C_KV * C_HD
C_SPLITS = (C_Q, C_K, C_K, C_K, C_K, C_K, C_K, 3 * C_HEADS)
C_IN = sum(C_SPLITS)

LANE = 128
VMEM_LIMIT = 48 * 1024 * 1024


def _split_cols(h, sizes):
    out, o = [], 0
    for n in sizes:
        out.append(h[..., o:o + n])
        o += n
    return out


def _rmsnorm_kernel(x_ref, g_ref, o_ref):
    x = x_ref[...]
    y = x * lax.rsqrt(jnp.mean(x * x, axis=-1, keepdims=True) + NORM_EPS)
    o_ref[...] = (y * g_ref[...]).astype(o_ref.dtype)


def _rmsnorm(x, g, out_dtype, tm=512):
    m, d = x.shape
    return pl.pallas_call(
        _rmsnorm_kernel,
        grid=(m // tm,),
        in_specs=[pl.BlockSpec((tm, d), lambda i: (i, 0)), pl.BlockSpec((1, d), lambda i: (0, 0))],
        out_specs=pl.BlockSpec((tm, d), lambda i: (i, 0)),
        out_shape=jax.ShapeDtypeStruct((m, d), out_dtype),
        compiler_params=pltpu.CompilerParams(dimension_semantics=("parallel",), vmem_limit_bytes=VMEM_LIMIT),
        name="rmsnorm",
    )(x, g.reshape(1, d))


def _mm_kernel(x_ref, w_ref, o_ref):
    o_ref[...] = jnp.dot(x_ref[...], w_ref[...], preferred_element_type=jnp.float32)


def _mm_res_kernel(x_ref, w_ref, r_ref, o_ref):
    o_ref[...] = r_ref[...] + jnp.dot(x_ref[...], w_ref[...], preferred_element_type=jnp.float32)


def _matmul(x, w, res=None, tm=1024, tn=512):
    m, k = x.shape
    n = w.shape[1]
    npad = -n % tn
    if npad:
        w = jnp.pad(w, ((0, 0), (0, npad)))
    nt = (n + npad) // tn
    in_specs = [pl.BlockSpec((tm, k), lambda i, j: (i, 0)), pl.BlockSpec((k, tn), lambda i, j: (0, j))]
    args = [x, w]
    kern = _mm_kernel
    if res is not None:
        in_specs.append(pl.BlockSpec((tm, tn), lambda i, j: (i, j)))
        args.append(res)
        kern = _mm_res_kernel
    out = pl.pallas_call(
        kern,
        grid=(m // tm, nt),
        in_specs=in_specs,
        out_specs=pl.BlockSpec((tm, tn), lambda i, j: (i, j)),
        out_shape=jax.ShapeDtypeStruct((m, n + npad), jnp.float32),
        compiler_params=pltpu.CompilerParams(dimension_semantics=("parallel", "parallel"),
                                             vmem_limit_bytes=VMEM_LIMIT),
        name="matmul",
    )(*args)
    return out[:, :n] if npad else out


def _masked_softmax(s, mask):
    s = jnp.where(mask, s, -jnp.inf)
    m = jnp.max(s, axis=-1, keepdims=True)
    m = jnp.where(jnp.isfinite(m), m, 0.0)
    p = jnp.exp(s - m)
    d = jnp.sum(p, axis=-1, keepdims=True)
    return p / jnp.where(d > 0, d, 1.0)


def _window_mask(qpos, kpos, window):
    d = qpos[..., :, None] - kpos[..., None, :]
    return (d >= 0) & (d <= window) & (kpos[..., None, :] >= 0)


def _sink_attention(q, k, v, mask, sinks):
    s = jnp.einsum('...tgrd,...sgd->...grts', q, k).astype(jnp.float32) * (A_HD ** -0.5)
    s = jnp.where(mask[..., None, None, :, :], s, -jnp.inf)
    sk = sinks.astype(jnp.float32)[:, :, None, None]
    m = jnp.maximum(jnp.max(s, axis=-1, keepdims=True), sk)
    p = jnp.exp(s - m)
    p = p / (jnp.sum(p, axis=-1, keepdims=True) + jnp.exp(sk - m))
    return jnp.einsum('...grts,...sgd->...tgrd', p, v)


def _mlstm_chunked(q, k, v, ig, fg, c0, n0, m0, chunk):
    f32 = jnp.float32
    bx, t = q.shape[0], q.shape[1]
    L = math.gcd(t, chunk)
    nc = t // L

    def chunks(a):
        return jnp.moveaxis(a.reshape(bx, nc, L, *a.shape[2:]), 1, 0)

    xs = (chunks(q), chunks(k * (B_DK ** -0.5)), chunks(v), chunks(ig), chunks(jax.nn.log_sigmoid(fg)))
    causal = jnp.tril(jnp.ones((L, L), dtype=bool))

    def step(carry, inp):
        c, n, m = carry
        qc, kc, vc, li, lf = inp
        b = jnp.cumsum(lf, axis=1)
        dlog = b[:, :, None, :] - b[:, None, :, :] + li[:, None, :, :]
        dlog = jnp.where(causal[None, :, :, None], dlog, -jnp.inf)
        inter = b + m[:, None, :]
        mrow = jnp.maximum(inter, jnp.max(dlog, axis=2))
        w = jnp.exp(dlog - mrow[:, :, None, :])
        a = jnp.exp(inter - mrow)
        sw = jnp.einsum('bthd,bjhd->btjh', qc, kc) * w
        num = jnp.einsum('btjh,bjhv->bthv', sw, vc) + a[..., None] * jnp.einsum('bthd,bhdv->bthv', qc, c)
        den = jnp.sum(sw, axis=2) + a * jnp.einsum('bthd,bhd->bth', qc, n)
        h = num / jnp.maximum(jnp.abs(den), jnp.exp(-mrow))[..., None]
        bl = b[:, -1]
        gj = bl[:, None, :] - b + li
        m_new = jnp.maximum(bl + m, jnp.max(gj, axis=1))
        wj = jnp.exp(gj - m_new[:, None, :])
        decay = jnp.exp(bl + m - m_new)
        c_new = decay[..., None, None] * c + jnp.einsum('bjh,bjhd,bjhv->bhdv', wj, kc, vc)
        n_new = decay[..., None] * n + jnp.einsum('bjh,bjhd->bhd', wj, kc)
        return (c_new, n_new, m_new), h

    (c, n, m), hs = lax.scan(step, (c0.astype(f32), n0.astype(f32), m0.astype(f32)), xs)
    return jnp.moveaxis(hs, 0, 1).reshape(bx, t, B_HEADS, B_DV), c, n, m


def _band_blocks(k, window, block):
    bx, s = k.shape[0], k.shape[1]
    nb, ns = s // block, window // block
    kp = jnp.pad(k, ((0, 0), (window, 0)) + ((0, 0),) * (k.ndim - 2))
    kb = kp.reshape(bx, nb + ns, block, *k.shape[2:])
    return jnp.concatenate([kb[:, i:i + nb] for i in range(ns + 1)], axis=2)


def _band_positions(nb, window, block):
    base = jnp.arange(nb)[:, None] * block
    return base + jnp.arange(block)[None, :], base - window + jnp.arange(window + block)[None, :]


def _mixer_ab(proj, x_res, swa_k, swa_v, c0, n0, m0, w_out, sinks, b_i, b_f, hnorm):
    s, db, t = SEQ, DEC_BATCH, DEC_SEQ
    sinks = sinks.reshape(A_KV, A_HEADS // A_KV)
    qa, ka, va, qb, kb, vb, ob, ib, fb = _split_cols(proj, AB_SPLITS)
    ib = ib + b_i
    fb = fb + b_f

    def part(a, lo, hi, shape):
        return a[lo:hi].reshape(shape)

    bx = 1
    nb = s // QBLK
    qa_p = part(qa, 0, s, (bx, nb, QBLK, A_KV, A_HEADS // A_KV, A_HD))
    ka_p = part(ka, 0, s, (bx, s, A_KV, A_HD))
    va_p = part(va, 0, s, (bx, s, A_KV, A_HD))
    qpos, kpos = _band_positions(nb, A_WIN, QBLK)
    oa_p = _sink_attention(qa_p, _band_blocks(ka_p, A_WIN, QBLK), _band_blocks(va_p, A_WIN, QBLK),
                           _window_mask(qpos, kpos, A_WIN)[None], sinks).reshape(s, A_Q)
    hb_p, cp, np_, mp = _mlstm_chunked(
        part(qb, 0, s, (bx, s, B_HEADS, B_DK)), part(kb, 0, s, (bx, s, B_HEADS, B_DK)),
        part(vb, 0, s, (bx, s, B_HEADS, B_DV)), part(ib, 0, s, (bx, s, B_HEADS)), part(fb, 0, s, (bx, s, B_HEADS)),
        jnp.zeros((bx, B_HEADS, B_DK, B_DV), jnp.float32), jnp.zeros((bx, B_HEADS, B_DK), jnp.float32),
        jnp.zeros((bx, B_HEADS), jnp.float32), B_CHUNK)
    n_tok = s + db * t
    qa_s = part(qa, s, n_tok, (db, t, A_KV, A_HEADS // A_KV, A_HD))
    ka_s = part(ka, s, n_tok, (db, t, A_KV, A_HD))
    va_s = part(va, s, n_tok, (db, t, A_KV, A_HD))
    wb = swa_k.shape[1]
    k_all = jnp.concatenate([swa_k, ka_s], axis=1)
    v_all = jnp.concatenate([swa_v, va_s], axis=1)
    kpos2 = PAST_LEN - wb + jnp.arange(wb + t)
    qpos2 = PAST_LEN + jnp.arange(t)
    oa_s = _sink_attention(qa_s, k_all, v_all, _window_mask(qpos2, kpos2, A_WIN)[None], sinks).reshape(db * t, A_Q)
    hb_s, cs, ns, ms = _mlstm_chunked(
        part(qb, s, n_tok, (db, t, B_HEADS, B_DK)), part(kb, s, n_tok, (db, t, B_HEADS, B_DK)),
        part(vb, s, n_tok, (db, t, B_HEADS, B_DV)), part(ib, s, n_tok, (db, t, B_HEADS)),
        part(fb, s, n_tok, (db, t, B_HEADS)), c0, n0, m0, B_CHUNK)
    oa = jnp.concatenate([oa_p, oa_s], axis=0)
    hb = jnp.concatenate([hb_p.reshape(s, B_HEADS, B_DV), hb_s.reshape(db * t, B_HEADS, B_DV)], axis=0)
    hn = hb * lax.rsqrt(jnp.mean(hb * hb, axis=-1, keepdims=True) + NORM_EPS)
    hn = hn.reshape(n_tok, B_V) * hnorm * jax.nn.sigmoid(ob)
    mix = jnp.concatenate([oa, hn], axis=-1).astype(jnp.bfloat16)
    x_new = _matmul(mix, w_out.astype(jnp.bfloat16), res=x_res)
    wbp = min(A_WIN, s)
    new = (ka_p[:, s - wbp:], va_p[:, s - wbp:], k_all[:, -wb:], v_all[:, -wb:], cp, np_, mp, cs, ns, ms)
    return x_new, new


def _compress(raw, pe, w1, w2):
    nc = raw.shape[0] // C_CMP
    blk = raw[:nc * C_CMP].reshape(nc, C_CMP, C_KV, C_HD) + pe[None, :, None, :]
    flat = jnp.swapaxes(blk, 1, 2).reshape(nc, C_KV, C_CMP * C_HD)
    return jax.nn.gelu(flat @ w1) @ w2


def _nsa_dense(q, qpos, kc, vc, ks, vs, kspos, nsb, kw, vw, kwpos, gate):
    f32 = jnp.float32
    t, g, r = q.shape[0], q.shape[1], q.shape[2]
    scale = C_HD ** -0.5
    nc = kc.shape[0]
    cend = (jnp.arange(nc) + 1) * C_CMP - 1
    cmask = (cend[None, :] <= qpos[:, None])[:, None, None, :]
    pc = _masked_softmax(jnp.einsum('tgrd,cgd->tgrc', q, kc).astype(f32) * scale, cmask)
    oc = jnp.einsum('tgrc,cgd->tgrd', pc, vc)
    cps = C_SLC // C_CMP
    imp = jnp.pad(pc.sum(axis=2), ((0, 0), (0, 0), (0, nsb * cps - nc)))
    imp = imp.reshape(t, g, nsb, cps).sum(axis=-1)
    jb = jnp.arange(nsb)[None, :]
    cur = (qpos // C_SLC)[:, None]
    forced = (jb == 0) | (jb == cur) | (jb == cur - 1)
    valid = (jb <= cur)[:, None, :]
    score = jnp.where(forced[:, None, :], jnp.inf, imp)
    score = jnp.where(valid, score, -jnp.inf)
    ntop = min(C_TOPN, nsb)
    tau = lax.top_k(score, ntop)[0][..., ntop - 1:ntop]
    sel = valid & (score >= tau)
    kblk = kspos // C_SLC
    smask = jnp.take(sel, kblk, axis=2) & (kspos[None, None, :] <= qpos[:, None, None])
    ss = jnp.einsum('tgrd,sgd->tgrs', q, ks).astype(f32) * scale
    ps = _masked_softmax(ss, smask[:, :, None, :])
    osl = jnp.einsum('tgrs,sgd->tgrd', ps, vs)
    wmask = _window_mask(qpos, kwpos, C_WIN)[:, None, None, :]
    pw = _masked_softmax(jnp.einsum('tgrd,wgd->tgrw', q, kw).astype(f32) * scale, wmask)
    ow = jnp.einsum('tgrw,wgd->tgrd', pw, vw)
    return gate[..., 0:1] * oc + gate[..., 1:2] * osl + gate[..., 2:3] * ow


def _mixer_c(proj, x_res, page_table, pool_ck, pool_cv, pool_sk, pool_sv, win_k, win_v, w_out,
             pe_k, w1_k, w2_k, pe_v, w1_v, w2_v):
    s, db, t = SEQ, DEC_BATCH, DEC_SEQ
    n_tok = s + db * t
    q, kcr, vcr, ksl, vsl, kw, vw, g = _split_cols(proj, C_SPLITS)
    g = jax.nn.sigmoid(g)
    rr = C_HEADS // C_KV

    def kv(a, lo, hi, lead):
        return a[lo:hi].reshape(*lead, C_KV, C_HD)

    q_p = q[:s].reshape(s, C_KV, rr, C_HD)
    g_p = g[:s].reshape(s, C_KV, rr, 3)
    kcr_p, vcr_p, ksl_p, vsl_p, kw_p, vw_p = (kv(a, 0, s, (s,)) for a in (kcr, vcr, ksl, vsl, kw, vw))
    kc = _compress(kcr_p, pe_k, w1_k, w2_k)
    vc = _compress(vcr_p, pe_v, w1_v, w2_v)
    nb = s // QBLK
    pos = jnp.arange(s)

    def prompt_block(i):
        qpos = i * QBLK + jnp.arange(QBLK)
        qq = lax.dynamic_slice_in_dim(q_p, i * QBLK, QBLK, axis=0)
        gg = lax.dynamic_slice_in_dim(g_p, i * QBLK, QBLK, axis=0)
        return _nsa_dense(qq, qpos, kc, vc, ksl_p, vsl_p, pos, s // C_SLC, kw_p, vw_p, pos, gg)

    o_p = lax.map(prompt_block, jnp.arange(nb)).reshape(s, C_Q)
    q_s = q[s:].reshape(db, t, C_KV, rr, C_HD)
    g_s = g[s:].reshape(db, t, C_KV, rr, 3)
    kcr_s, vcr_s, ksl_s, vsl_s, kw_s, vw_s = (kv(a, s, n_tok, (db, t)) for a in (kcr, vcr, ksl, vsl, kw, vw))
    L = PAST_LEN + t
    nsb = -(-L // C_SLC)
    wb = win_k.shape[1]
    kw_all = jnp.concatenate([win_k, kw_s], axis=1)
    vw_all = jnp.concatenate([win_v, vw_s], axis=1)
    kwpos = PAST_LEN - wb + jnp.arange(wb + t)
    qpos_s = PAST_LEN + jnp.arange(t)
    kspos = jnp.arange(L)

    def sample_seq(args):
        pt, q1, kcr1, vcr1, ks1, vs1, kw1, vw1, g1 = args
        raw_k = jnp.concatenate([pool_ck[pt].reshape(PAST_LEN, C_KV, C_HD), kcr1], axis=0)
        raw_v = jnp.concatenate([pool_cv[pt].reshape(PAST_LEN, C_KV, C_HD), vcr1], axis=0)
        kc1 = _compress(raw_k, pe_k, w1_k, w2_k)
        vc1 = _compress(raw_v, pe_v, w1_v, w2_v)
        ks_all = jnp.concatenate([pool_sk[pt].reshape(PAST_LEN, C_KV, C_HD), ks1], axis=0)
        vs_all = jnp.concatenate([pool_sv[pt].reshape(PAST_LEN, C_KV, C_HD), vs1], axis=0)
        return _nsa_dense(q1, qpos_s, kc1, vc1, ks_all, vs_all, kspos, nsb, kw1, vw1, kwpos, g1)

    o_s = lax.map(sample_seq, (page_table, q_s, kcr_s, vcr_s, ksl_s, vsl_s, kw_all, vw_all, g_s))
    o = jnp.concatenate([o_p, o_s.reshape(db * t, C_Q)], axis=0).astype(jnp.bfloat16)
    x_new = _matmul(o, w_out.astype(jnp.bfloat16), res=x_res)
    wbp = min(C_WIN, s)
    new = (kcr_p[None], vcr_p[None], ksl_p[None], vsl_p[None], kcr_s, vcr_s, ksl_s, vsl_s,
           kw_p[None, s - wbp:], vw_p[None, s - wbp:], kw_all[:, -wb:], vw_all[:, -wb:])
    return x_new, new


def _peer(hn, x_res, w_q, keys1, keys2, u_tab, v_tab, blk=512):
    n_tok = hn.shape[0]
    q = _matmul(hn, w_q.astype(jnp.bfloat16)).reshape(n_tok, P_HEADS, 2, P_DKH)
    s1 = jnp.einsum('thd,hkd->thk', q[:, :, 0], keys1)
    s2 = jnp.einsum('thd,hkd->thk', q[:, :, 1], keys2)
    v1 = lax.top_k(s1, P_TOPK)[0]
    v2 = lax.top_k(s2, P_TOPK)[0]
    cand = (v1[..., :, None] + v2[..., None, :]).reshape(n_tok, P_HEADS, P_TOPK * P_TOPK)
    sv = lax.top_k(cand, P_TOPK)[0]
    tau = sv[..., P_TOPK - 1]
    mx = sv[..., 0]
    z = jnp.sum(jnp.exp(sv - mx[..., None]), axis=-1)
    ub = u_tab.astype(jnp.bfloat16)
    vb = v_tab.astype(jnp.bfloat16)

    def body(args):
        h, a1, a2, ta, m_, z_ = args
        sums = a1[:, :, :, None] + a2[:, :, None, :]
        gsel = jnp.where(sums >= ta[:, :, None, None], jnp.exp(sums - m_[:, :, None, None]) / z_[:, :, None, None], 0.0)
        gmat = gsel.sum(axis=1).reshape(-1, P_EXPERTS)
        act = jax.nn.gelu(jnp.dot(h, ub.T, preferred_element_type=jnp.float32))
        return jnp.dot((gmat * act).astype(jnp.bfloat16), vb, preferred_element_type=jnp.float32)

    nb = n_tok // blk

    def rs(a):
        return a.reshape(nb, blk, *a.shape[1:])

    out = lax.map(body, (rs(hn), rs(s1), rs(s2), rs(tau), rs(mx), rs(z))).reshape(n_tok, D_MODEL)
    return x_res + out


def kernel(x_prompt, x_sample, cache_swa_k, cache_swa_v, state_mlstm_c, state_mlstm_n, state_mlstm_m,
           cache_nsa_cmp_k, cache_nsa_cmp_v, cache_nsa_slc_k, cache_nsa_slc_v, cache_nsa_win_k, cache_nsa_win_v,
           page_table, norm_mix, norm_ffn, norm_final, ab_w_in, ab_w_out, ab_sinks, ab_b_igate, ab_b_fgate,
           ab_hnorm, c_w_in, c_w_out, c_pe_k, c_w1_k, c_w2_k, c_pe_v, c_w1_v, c_w2_v,
           peer_w_q, peer_keys1, peer_keys2, peer_u, peer_v):
    bf16 = jnp.bfloat16
    s, db, t = SEQ, DEC_BATCH, DEC_SEQ
    x = jnp.concatenate([x_prompt.reshape(s, D_MODEL), x_sample.reshape(db * t, D_MODEL)], axis=0)
    ab_new, c_new = [], []
    for layer in range(DEPTH):
        hn = _rmsnorm(x, norm_mix[layer], bf16)
        if layer % 2 == 0:
            a = layer // 2
            proj = _matmul(hn, ab_w_in[a].astype(bf16))
            x, st = _mixer_ab(proj, x, cache_swa_k[a], cache_swa_v[a], state_mlstm_c[a], state_mlstm_n[a],
                              state_mlstm_m[a], ab_w_out[a], ab_sinks[a], ab_b_igate[a], ab_b_fgate[a], ab_hnorm[a])
            ab_new.append(st)
        else:
            c = layer // 2
            proj = _matmul(hn, c_w_in[c].astype(bf16))
            x, st = _mixer_c(proj, x, page_table, cache_nsa_cmp_k[c], cache_nsa_cmp_v[c], cache_nsa_slc_k[c],
                             cache_nsa_slc_v[c], cache_nsa_win_k[c], cache_nsa_win_v[c], c_w_out[c],
                             c_pe_k[c], c_w1_k[c], c_w2_k[c], c_pe_v[c], c_w1_v[c], c_w2_v[c])
            c_new.append(st)
        hn = _rmsnorm(x, norm_ffn[layer], bf16)
        x = _peer(hn, x, peer_w_q[layer], peer_keys1[layer], peer_keys2[layer], peer_u[layer], peer_v[layer])
    y = _rmsnorm(x, norm_final, jnp.float32)
    y_prompt = y[:s].reshape(1, s, D_MODEL)
    y_sample = y[s:].reshape(db, t, D_MODEL)
    ab_out = tuple(jnp.stack([st[i] for st in ab_new]) for i in range(10))
    c_out = tuple(jnp.stack([st[i] for st in c_new]) for i in range(12))
    return (y_prompt, y_sample) + ab_out + c_out
```
